```python
import math
import jax, jax.numpy as jnp
from jax import lax
import numpy as np

D_MODEL = 1024
BATCH = 4
SEQ = 8192
DEPTH = 4

CHUNK = 64
EPS = 1e-6
NEG_INF = -1e30
N_BRANCH = 3
BR_WIDTH = 512

SGU_BLOCK = 128
SGU_GROUPS = 8
SGU_GROUP_DIM = BR_WIDTH // SGU_GROUPS

MLA_HEADS = 8
MLA_NOPE = 64
MLA_ROPE = 32
MLA_V = 64
MLA_QK = MLA_NOPE + MLA_ROPE
MLA_Q_RANK = 256
MLA_KV_RANK = 128
ROPE_BASE = 10000.0
Q_BLOCK = 128

CA_HEADS = 8
CA_HEAD_DIM = BR_WIDTH // CA_HEADS
LEFT_CHUNKS = 8
BAND = (LEFT_CHUNKS + 1) * CHUNK
REL_CLIP = 128

IN_WIDTHS = (BR_WIDTH, BR_WIDTH, BR_WIDTH,
             MLA_Q_RANK, MLA_KV_RANK, MLA_ROPE, BR_WIDTH,
             BR_WIDTH, BR_WIDTH, BR_WIDTH, BR_WIDTH,
             N_BRANCH * D_MODEL)
D_IN = sum(IN_WIDTHS)

kernel_name = "hybrid_sgu_mla_chunkattn_streaming"


def rmsnorm(x, g):
    xf = x.astype(jnp.float32)
    y = xf * lax.rsqrt(jnp.mean(xf * xf, axis=-1, keepdims=True) + EPS)
    return (y * g.astype(jnp.float32)).astype(x.dtype)


def layernorm(x, g, b):
    xf = x.astype(jnp.float32)
    mu = jnp.mean(xf, axis=-1, keepdims=True)
    xc = xf - mu
    y = xc * lax.rsqrt(jnp.mean(xc * xc, axis=-1, keepdims=True) + EPS)
    return (y * g.astype(jnp.float32) + b.astype(jnp.float32)).astype(x.dtype)


def apply_rope(x, pos):
    half = x.shape[-1] // 2
    inv = ROPE_BASE ** (-jnp.arange(half, dtype=jnp.float32) / half)
    ang = pos.astype(jnp.float32)[:, None] * inv[None, :]
    cos = jnp.cos(ang)[:, None, :]
    sin = jnp.sin(ang)[:, None, :]
    xf = x.astype(jnp.float32)
    x1, x2 = xf[..., :half], xf[..., half:]
    return jnp.concatenate([x1 * cos - x2 * sin, x1 * sin + x2 * cos], axis=-1).astype(x.dtype)


def sgu_mixer(u, v, ln_g, ln_b, w_s, b_s):
    B, S, _ = u.shape
    nb = S // SGU_BLOCK
    v = layernorm(v, ln_g, ln_b)
    vb = v.reshape(B, nb, SGU_BLOCK, SGU_GROUPS, SGU_GROUP_DIM)
    tri = jnp.tril(jnp.ones((SGU_BLOCK, SGU_BLOCK), dtype=bool))
    ws = jnp.where(tri[None], w_s, 0.0).astype(v.dtype)
    mixed = jnp.einsum('gts,bnsgc->bntgc', ws, vb) + b_s.T.astype(v.dtype)[None, None, :, :, None]
    return u * mixed.reshape(B, S, BR_WIDTH)


def mla_mixer(q_down, kv_down, k_rope_in, q_norm_g, kv_norm_g, w_uq, w_ukv, pos):
    B, S, _ = q_down.shape
    cq = rmsnorm(q_down, q_norm_g)
    q = (cq @ w_uq).reshape(B, S, MLA_HEADS, MLA_QK)
    q = jnp.concatenate([q[..., :MLA_NOPE], apply_rope(q[..., MLA_NOPE:], pos)], axis=-1)
    ckv = rmsnorm(kv_down, kv_norm_g)
    kv = (ckv @ w_ukv).reshape(B, S, MLA_HEADS, MLA_NOPE + MLA_V)
    k_nope, v = kv[..., :MLA_NOPE], kv[..., MLA_NOPE:]
    k_r = apply_rope(k_rope_in[:, :, None, :], pos)
    k = jnp.concatenate([k_nope, jnp.broadcast_to(k_r, (B, S, MLA_HEADS, MLA_ROPE))], axis=-1)
    scale = MLA_QK ** -0.5
    nqb = S // Q_BLOCK
    qb = q.reshape(B, nqb, Q_BLOCK, MLA_HEADS, MLA_QK).transpose(1, 0, 2, 3, 4)
    key_chunk = jnp.arange(S) // CHUNK

    def block(args):
        qi, bi = args
        s = jnp.einsum('bqhd,bkhd->bhqk', qi, k).astype(jnp.float32) * scale
        q_chunk = (bi * Q_BLOCK + jnp.arange(Q_BLOCK)) // CHUNK
        mask = key_chunk[None, :] <= q_chunk[:, None]
        s = jnp.where(mask[None, None], s, NEG_INF)
        p = jax.nn.softmax(s, axis=-1).astype(v.dtype)
        return jnp.einsum('bhqk,bkhd->bqhd', p, v)

    o = lax.map(block, (qb, jnp.arange(nqb)))
    return o.transpose(1, 0, 2, 3, 4).reshape(B, S, MLA_HEADS * MLA_V)


def chunk_band_mixer(q, k, v, rel_table):
    B, S, _ = q.shape
    nc = S // CHUNK
    pad = LEFT_CHUNKS * CHUNK
    q = q.reshape(B, S, CA_HEADS, CA_HEAD_DIM)
    k = k.reshape(B, S, CA_HEADS, CA_HEAD_DIM)
    v = v.reshape(B, S, CA_HEADS, CA_HEAD_DIM)
    kp = jnp.pad(k, ((0, 0), (pad, 0), (0, 0), (0, 0)))
    vp = jnp.pad(v, ((0, 0), (pad, 0), (0, 0), (0, 0)))
    qc = q.reshape(B, nc, CHUNK, CA_HEADS, CA_HEAD_DIM).transpose(1, 0, 2, 3, 4)
    i = jnp.arange(CHUNK)
    j = jnp.arange(BAND)
    dist = i[:, None] + pad - j[None, :]
    idx = jnp.clip(dist, -REL_CLIP, REL_CLIP) + REL_CLIP
    bias = rel_table[:, idx].astype(jnp.float32)
    scale = CA_HEAD_DIM ** -0.5

    def chunk(args):
        qi, ci = args
        kb = lax.dynamic_slice_in_dim(kp, ci * CHUNK, BAND, axis=1)
        vb = lax.dynamic_slice_in_dim(vp, ci * CHUNK, BAND, axis=1)
        s = jnp.einsum('bqhd,bkhd->bhqk', qi, kb).astype(jnp.float32) * scale + bias[None]
        valid = j >= (LEFT_CHUNKS - ci) * CHUNK
        s = jnp.where(valid[None, None, None, :], s, NEG_INF)
        p = jax.nn.softmax(s, axis=-1).astype(vb.dtype)
        return jnp.einsum('bhqk,bkhd->bqhd', p, vb)

    o = lax.map(chunk, (qc, jnp.arange(nc)))
    return o.transpose(1, 0, 2, 3, 4).reshape(B, S, BR_WIDTH)


def setup_inputs(seed: int = 0) -> dict:
    key = jax.random.key(seed)
    ks = jax.random.split(key, 16)
    f32 = jnp.float32
    L, D = DEPTH, D_MODEL
    nrm = lambda k, shape, s: jax.random.normal(k, shape, f32) * s
    return {
        "x": jax.random.normal(ks[0], (BATCH, SEQ, D), f32),
        "w_in": nrm(ks[1], (L, D, D_IN), D ** -0.5),
        "pre_g": 1.0 + nrm(ks[2], (L, D), 0.1),
        "post_g": 1.0 + nrm(ks[3], (L, D), 0.1),
        "sgu_ln_g": 1.0 + nrm(ks[4], (L, BR_WIDTH), 0.1),
        "sgu_ln_b": nrm(ks[5], (L, BR_WIDTH), 0.02),
        "sgu_w": nrm(ks[6], (L, SGU_GROUPS, SGU_BLOCK, SGU_BLOCK), SGU_BLOCK ** -0.5),
        "sgu_b": 1.0 + nrm(ks[7], (L, SGU_GROUPS, SGU_BLOCK), 0.1),
        "mla_q_norm_g": 1.0 + nrm(ks[8], (L, MLA_Q_RANK), 0.1),
        "mla_kv_norm_g": 1.0 + nrm(ks[9], (L, MLA_KV_RANK), 0.1),
        "mla_w_uq": nrm(ks[10], (L, MLA_Q_RANK, MLA_HEADS * MLA_QK), MLA_Q_RANK ** -0.5),
        "mla_w_ukv": nrm(ks[11], (L, MLA_KV_RANK, MLA_HEADS * (MLA_NOPE + MLA_V)), MLA_KV_RANK ** -0.5),
        "ca_rel_bias": nrm(ks[12], (L, CA_HEADS, 2 * REL_CLIP + 1), 0.5),
        "w_branch": nrm(ks[13], (L, N_BRANCH, BR_WIDTH, D), BR_WIDTH ** -0.5),
        "gate_b": nrm(ks[14], (L, N_BRANCH, D), 0.1),
        "w_out": nrm(ks[15], (L, D, D), D ** -0.5),
    }


def reference(x, w_in, pre_g, post_g, sgu_ln_g, sgu_ln_b, sgu_w, sgu_b,
              mla_q_norm_g, mla_kv_norm_g, mla_w_uq, mla_w_ukv, ca_rel_bias,
              w_branch, gate_b, w_out):
    B, S, D = x.shape
    pos = jnp.arange(S)
    offsets = [0]
    for w in IN_WIDTHS:
        offsets.append(offsets[-1] + w)
    for l in range(DEPTH):
        xn = rmsnorm(x, pre_g[l])
        proj = xn @ w_in[l]
        (u_a, v_a, z_a, qd_b, kvd_b, kr_b, z_b,
         q_c, k_c, v_c, z_c, g_logits) = [proj[..., offsets[n]:offsets[n + 1]] for n in range(len(IN_WIDTHS))]
        y_a = sgu_mixer(u_a, v_a, sgu_ln_g[l], sgu_ln_b[l], sgu_w[l], sgu_b[l]) * jax.nn.silu(z_a)
        y_b = mla_mixer(qd_b, kvd_b, kr_b, mla_q_norm_g[l], mla_kv_norm_g[l],
                        mla_w_uq[l], mla_w_ukv[l], pos) * jax.nn.silu(z_b)
        y_c = chunk_band_mixer(q_c, k_c, v_c, ca_rel_bias[l]) * jax.nn.silu(z_c)
        ys = jnp.stack([y_a, y_b, y_c], axis=2)
        br = jnp.einsum('bsnc,ncd->bsnd', ys, w_branch[l])
        gates = jax.nn.sigmoid(g_logits.reshape(B, S, N_BRANCH, D) + gate_b[l])
        merged = jnp.sum(gates * br, axis=2)
        x = x + rmsnorm(merged @ w_out[l], post_g[l])
    return x
```

```python
import functools

import jax
import jax.numpy as jnp
from jax import lax
from jax.experimental import pallas as pl
from jax.experimental.pallas import tpu as pltpu

F32 = jnp.float32
BF16 = jnp.bfloat16

D_MODEL = 1024
CHUNK = 64
EPS = 1e-6
NEG_INF = -1e30
N_BRANCH = 3
BR_WIDTH = 512

SGU_BLOCK = 128
SGU_GROUPS = 8
SGU_GROUP_DIM = BR_WIDTH // SGU_GROUPS

MLA_HEADS = 8
MLA_NOPE = 64
MLA_ROPE = 32
MLA_V = 64
MLA_QK = MLA_NOPE + MLA_ROPE
MLA_Q_RANK = 256
MLA_KV_RANK = 128
ROPE_BASE = 10000.0

CA_HEADS = 8
CA_HEAD_DIM = BR_WIDTH // CA_HEADS
LEFT_CHUNKS = 8
REL_CLIP = 128

LANES = 128
HEAD_PAIRS = MLA_HEADS // 2

COL_SGU = 0
COL_MLA = 1536
COL_ZB = 2048
COL_BAND = 2560
COL_GATE = 4608
W_COLS = 7680

FRONT_TM = 256
BACK_TM = 512
MLA_TQ = 512
BAND_TQ = 256
BAND_KEYS = 3 * BAND_TQ

VMEM_LIMIT_FRONT = 56 * 1024 * 1024
VMEM_LIMIT_OTHER = 48 * 1024 * 1024


def _const_spec(shape):
    zeros = (0,) * len(shape)
    return pl.BlockSpec(shape, lambda *_: zeros, pipeline_mode=pl.Buffered(1))


def _dot(a, b):
    return jnp.dot(a, b, preferred_element_type=F32)


def _dot_nt(a, b):
    return lax.dot_general(a, b, (((1,), (1,)), ((), ())), preferred_element_type=F32)


def _silu(z):
    return z * jax.nn.sigmoid(z)


def _front_kernel(x_ref, w_ref, preg_ref, lng_ref, lnb_ref, sguw_ref, sgub_ref, qg_ref, kvg_ref,
                  wuq_ref, wuqs_ref, wk_ref, wv_ref, gateb_ref, cos_ref, sin_ref,
                  ya_ref, q_ref, k_ref, v_ref, szb_ref, qc_ref, kc_ref, vc_ref, szc_ref, g_ref):
    tm = x_ref.shape[0]
    x = x_ref[...]
    xn = x * lax.rsqrt(jnp.mean(x * x, axis=-1, keepdims=True) + EPS)
    xn = (xn * preg_ref[...]).astype(BF16)

    pa = _dot(xn, w_ref[:, COL_SGU:COL_SGU + 3 * BR_WIDTH])
    u = pa[:, 0:BR_WIDTH]
    v = pa[:, BR_WIDTH:2 * BR_WIDTH]
    za = pa[:, 2 * BR_WIDTH:3 * BR_WIDTH]
    mu = jnp.mean(v, axis=-1, keepdims=True)
    vc = v - mu
    vn = vc * lax.rsqrt(jnp.mean(vc * vc, axis=-1, keepdims=True) + EPS)
    vn = (vn * lng_ref[...] + lnb_ref[...]).astype(BF16)

    row = lax.broadcasted_iota(jnp.int32, (SGU_BLOCK, SGU_BLOCK), 0)
    col = lax.broadcasted_iota(jnp.int32, (SGU_BLOCK, SGU_BLOCK), 1)
    tri = row >= col
    ws = [jnp.where(tri, sguw_ref[g], 0.0).astype(BF16) for g in range(SGU_GROUPS)]
    low_half = lax.broadcasted_iota(jnp.int32, (SGU_BLOCK, LANES), 1) < SGU_GROUP_DIM
    blocks = []
    for blk in range(tm // SGU_BLOCK):
        vb = vn[blk * SGU_BLOCK:(blk + 1) * SGU_BLOCK, :]
        tiles = []
        for j in range(BR_WIDTH // LANES):
            vt = vb[:, j * LANES:(j + 1) * LANES]
            tiles.append(jnp.where(low_half, _dot(ws[2 * j], vt), _dot(ws[2 * j + 1], vt)))
        blocks.append(jnp.concatenate(tiles, axis=1) + sgub_ref[...])
    mixed = jnp.concatenate(blocks, axis=0)
    ya_ref[...] = ((u * mixed) * _silu(za)).astype(BF16)

    pb = _dot(xn, w_ref[:, COL_MLA:COL_MLA + 512])
    qd = pb[:, 0:MLA_Q_RANK]
    kvd = pb[:, MLA_Q_RANK:MLA_Q_RANK + MLA_KV_RANK]
    kr = pb[:, MLA_Q_RANK + MLA_KV_RANK:512]
    cos_t = cos_ref[...]
    sin_t = sin_ref[...]
    cq = qd * lax.rsqrt(jnp.mean(qd * qd, axis=-1, keepdims=True) + EPS)
    cq = (cq * qg_ref[...]).astype(BF16)
    qf = _dot(cq, wuq_ref[...])
    qs = _dot(cq, wuqs_ref[...])
    ckv = kvd * lax.rsqrt(jnp.mean(kvd * kvd, axis=-1, keepdims=True) + EPS)
    ckv = (ckv * kvg_ref[...]).astype(BF16)
    kf = _dot(ckv, wk_ref[...])
    kr_roped = kr * cos_t + pltpu.roll(kr, LANES - MLA_ROPE, 1) * sin_t
    scale = MLA_QK ** -0.5
    for h in range(MLA_HEADS):
        sl = slice(h * LANES, (h + 1) * LANES)
        q_ref[:, sl] = ((qf[:, sl] * cos_t + qs[:, sl] * sin_t) * scale).astype(BF16)
        k_ref[:, sl] = (kf[:, sl] + kr_roped).astype(BF16)
    v_ref[...] = _dot(ckv, wv_ref[...]).astype(BF16)
    szb_ref[...] = _silu(_dot(xn, w_ref[:, COL_ZB:COL_ZB + BR_WIDTH])).astype(BF16)

    pc = _dot(xn, w_ref[:, COL_BAND:COL_BAND + 4 * BR_WIDTH])
    qc_ref[...] = (pc[:, 0:BR_WIDTH] * (CA_HEAD_DIM ** -0.5)).astype(BF16)
    kc_ref[...] = pc[:, BR_WIDTH:2 * BR_WIDTH].astype(BF16)
    vc_ref[...] = pc[:, 2 * BR_WIDTH:3 * BR_WIDTH].astype(BF16)
    szc_ref[...] = _silu(pc[:, 3 * BR_WIDTH:4 * BR_WIDTH]).astype(BF16)

    pg = _dot(xn, w_ref[:, COL_GATE:COL_GATE + N_BRANCH * D_MODEL]) + gateb_ref[...]
    g_ref[...] = jax.nn.sigmoid(pg).astype(BF16)


def _front_call(x2, w, preg, lng, lnb, sguw, sgub_full, qg, kvg, wuq, wuqs, wk, wv, gateb, cos_t, sin_t,
                seq_len):
    n_tok = x2.shape[0]
    tm = FRONT_TM
    tiles_per_seq = seq_len // tm
    tok = lambda width: pl.BlockSpec((tm, width), lambda i: (i, 0))
    pos = pl.BlockSpec((tm, LANES), lambda i: (i % tiles_per_seq, 0))
    out_widths = (BR_WIDTH, MLA_HEADS * LANES, MLA_HEADS * LANES, BR_WIDTH, BR_WIDTH,
                  BR_WIDTH, BR_WIDTH, BR_WIDTH, BR_WIDTH, N_BRANCH * D_MODEL)
    return pl.pallas_call(
        _front_kernel,
        grid=(n_tok // tm,),
        in_specs=[tok(D_MODEL), _const_spec(w.shape), _const_spec(preg.shape), _const_spec(lng.shape),
                  _const_spec(lnb.shape), _const_spec(sguw.shape), _const_spec(sgub_full.shape),
                  _const_spec(qg.shape), _const_spec(kvg.shape), _const_spec(wuq.shape),
                  _const_spec(wuqs.shape), _const_spec(wk.shape), _const_spec(wv.shape),
                  _const_spec(gateb.shape), pos, pos],
        out_specs=[tok(wd) for wd in out_widths],
        out_shape=[jax.ShapeDtypeStruct((n_tok, wd), BF16) for wd in out_widths],
        compiler_params=pltpu.CompilerParams(dimension_semantics=("arbitrary",),
                                             vmem_limit_bytes=VMEM_LIMIT_FRONT),
        name="front",
    )(x2, w, preg, lng, lnb, sguw, sgub_full, qg, kvg, wuq, wuqs, wk, wv, gateb, cos_t, sin_t)


def _mla_kernel(q_ref, k_ref, v_ref, sz_ref, o_ref):
    t = MLA_TQ
    i = pl.program_id(2)
    qs = [q_ref[0, :, h * LANES:(h + 1) * LANES] for h in range(2)]
    row_chunk = lax.broadcasted_iota(jnp.int32, (t, t), 0) // CHUNK
    col_chunk = lax.broadcasted_iota(jnp.int32, (t, t), 1) // CHUNK
    diag_mask = col_chunk <= row_chunk

    def step(j, carry, masked):
        off = pl.multiple_of(j * t, t)
        vblk = v_ref[0, pl.ds(off, t), :]
        new = []
        for h in range(2):
            m, l, acc = carry[h]
            kblk = k_ref[0, pl.ds(off, t), h * LANES:(h + 1) * LANES]
            s = _dot_nt(qs[h], kblk)
            if masked:
                s = jnp.where(diag_mask, s, NEG_INF)
            m_new = jnp.maximum(m, jnp.max(s, axis=-1, keepdims=True))
            alpha = jnp.exp(m - m_new)
            p = jnp.exp(s - m_new)
            l_new = alpha * l + jnp.sum(p, axis=-1, keepdims=True)
            acc_new = alpha * acc + _dot(p.astype(BF16), vblk)
            new.append((m_new, l_new, acc_new))
        return tuple(new)

    init = tuple((jnp.full((t, 1), NEG_INF, F32), jnp.zeros((t, 1), F32), jnp.zeros((t, LANES), F32))
                 for _ in range(2))
    carry = lax.fori_loop(0, i, lambda j, c: step(j, c, False), init)
    carry = step(i, carry, True)
    (_, l0, a0), (_, l1, a1) = carry
    low_half = lax.broadcasted_iota(jnp.int32, (t, LANES), 1) < MLA_V
    o = jnp.where(low_half, a0 / l0, a1 / l1)
    o_ref[0] = (o * sz_ref[0].astype(F32)).astype(BF16)


def _mla_call(q, k, v, szb):
    batch, seq_len, _ = q.shape
    t = MLA_TQ
    return pl.pallas_call(
        _mla_kernel,
        grid=(batch, HEAD_PAIRS, seq_len // t),
        in_specs=[pl.BlockSpec((1, t, 2 * LANES), lambda b, hp, i: (b, i, hp)),
                  pl.BlockSpec((1, seq_len, 2 * LANES), lambda b, hp, i: (b, 0, hp)),
                  pl.BlockSpec((1, seq_len, LANES), lambda b, hp, i: (b, 0, hp)),
                  pl.BlockSpec((1, t, LANES), lambda b, hp, i: (b, i, hp))],
        out_specs=pl.BlockSpec((1, t, LANES), lambda b, hp, i: (b, i, hp)),
        out_shape=jax.ShapeDtypeStruct((batch, seq_len, BR_WIDTH), BF16),
        compiler_params=pltpu.CompilerParams(dimension_semantics=("arbitrary",) * 3,
                                             vmem_limit_bytes=VMEM_LIMIT_OTHER),
        name="mla",
    )(q, k, v, szb)


def _band_kernel(q_ref, k2_ref, k1_ref, k0_ref, v2_ref, v1_ref, v0_ref, bias_ref, sz_ref, o_ref):
    t = BAND_TQ
    i = pl.program_id(1)
    col = lax.broadcasted_iota(jnp.int32, (t, BAND_KEYS), 1)
    first_valid = jnp.maximum(2 - i, 0) * t
    col_valid = col >= first_valid
    lane = lax.broadcasted_iota(jnp.int32, (t, LANES), 1)
    low_half = lane < CA_HEAD_DIM
    outs = []
    for j in range(BR_WIDTH // LANES):
        sl = slice(j * LANES, (j + 1) * LANES)
        qp = q_ref[0, :, sl]
        kcat = jnp.concatenate([k2_ref[0, :, sl], k1_ref[0, :, sl], k0_ref[0, :, sl]], axis=0)
        vcat = jnp.concatenate([v2_ref[0, :, sl], v1_ref[0, :, sl], v0_ref[0, :, sl]], axis=0)
        pair = []
        for hh in range(2):
            keep = low_half if hh == 0 else jnp.logical_not(low_half)
            qm = jnp.where(keep, qp, jnp.zeros_like(qp))
            s = _dot_nt(qm, kcat) + bias_ref[2 * j + hh]
            s = jnp.where(col_valid, s, NEG_INF)
            m = jnp.max(s, axis=-1, keepdims=True)
            p = jnp.exp(s - m)
            l = jnp.sum(p, axis=-1, keepdims=True)
            pair.append(_dot(p.astype(BF16), vcat) / l)
        outs.append(jnp.where(low_half, pair[0], pair[1]))
    o = jnp.concatenate(outs, axis=1)
    o_ref[0] = (o * sz_ref[0].astype(F32)).astype(BF16)


def _band_call(qc, kc, vc, bias, szc):
    batch, seq_len, _ = qc.shape
    t = BAND_TQ
    cur = pl.BlockSpec((1, t, BR_WIDTH), lambda b, i: (b, i, 0))
    prev1 = pl.BlockSpec((1, t, BR_WIDTH), lambda b, i: (b, jnp.maximum(i - 1, 0), 0))
    prev2 = pl.BlockSpec((1, t, BR_WIDTH), lambda b, i: (b, jnp.maximum(i - 2, 0), 0))
    return pl.pallas_call(
        _band_kernel,
        grid=(batch, seq_len // t),
        in_specs=[cur, prev2, prev1, cur, prev2, prev1, cur, _const_spec(bias.shape), cur],
        out_specs=cur,
        out_shape=jax.ShapeDtypeStruct((batch, seq_len, BR_WIDTH), BF16),
        compiler_params=pltpu.CompilerParams(dimension_semantics=("arbitrary",) * 2,
                                             vmem_limit_bytes=VMEM_LIMIT_OTHER),
        name="band",
    )(qc, kc, kc, kc, vc, vc, vc, bias, szc)


def _back_kernel(x_ref, ya_ref, yb_ref, yc_ref, g_ref, wbr_ref, wout_ref, postg_ref, o_ref):
    merged = None
    for n, y_ref in enumerate((ya_ref, yb_ref, yc_ref)):
        br = _dot(y_ref[...], wbr_ref[n])
        term = g_ref[:, n * D_MODEL:(n + 1) * D_MODEL].astype(F32) * br
        merged = term if merged is None else merged + term
    out = _dot(merged.astype(BF16), wout_ref[...])
    y = out * lax.rsqrt(jnp.mean(out * out, axis=-1, keepdims=True) + EPS)
    o_ref[...] = x_ref[...] + y * postg_ref[...]


def _back_call(x2, ya, yb, yc, gates, wbr, wout, postg):
    n_tok = x2.shape[0]
    tm = BACK_TM
    tok = lambda width: pl.BlockSpec((tm, width), lambda i: (i, 0))
    return pl.pallas_call(
        _back_kernel,
        grid=(n_tok // tm,),
        in_specs=[tok(D_MODEL), tok(BR_WIDTH), tok(BR_WIDTH), tok(BR_WIDTH), tok(N_BRANCH * D_MODEL),
                  _const_spec(wbr.shape), _const_spec(wout.shape), _const_spec(postg.shape)],
        out_specs=tok(D_MODEL),
        out_shape=jax.ShapeDtypeStruct((n_tok, D_MODEL), F32),
        compiler_params=pltpu.CompilerParams(dimension_semantics=("arbitrary",),
                                             vmem_limit_bytes=VMEM_LIMIT_OTHER),
        name="back",
    )(x2, ya, yb, yc, gates, wbr, wout, postg)


def _rope_partner(w):
    half = w.shape[-1] // 2
    return jnp.concatenate([-w[..., half:], w[..., :half]], axis=-1)


def _prep_w_in(w_in):
    n_layers, d, _ = w_in.shape
    off_kr = 3 * BR_WIDTH + MLA_Q_RANK + MLA_KV_RANK
    off_zb = off_kr + MLA_ROPE
    w_kr = w_in[:, :, off_kr:off_zb]
    rope_tile = jnp.concatenate([jnp.zeros((n_layers, d, LANES - 2 * MLA_ROPE), w_in.dtype),
                                 w_kr, _rope_partner(w_kr)], axis=-1)
    w = jnp.concatenate([w_in[:, :, :off_kr], rope_tile, w_in[:, :, off_zb:]], axis=-1)
    assert w.shape[-1] == W_COLS
    return w.astype(BF16)


def _prep_w_uq(w_uq):
    n_layers = w_uq.shape[0]
    w = w_uq.reshape(n_layers, MLA_Q_RANK, MLA_HEADS, MLA_QK)
    nope, rope = w[..., :MLA_NOPE], w[..., MLA_NOPE:]
    pad = jnp.zeros(w.shape[:3] + (LANES - MLA_QK,), w.dtype)
    main = jnp.concatenate([nope, rope, pad], axis=-1)
    partner = jnp.concatenate([jnp.zeros_like(nope), _rope_partner(rope), pad], axis=-1)
    shape = (n_layers, MLA_Q_RANK, MLA_HEADS * LANES)
    return main.reshape(shape).astype(BF16), partner.reshape(shape).astype(BF16)


def _prep_w_ukv(w_ukv):
    n_layers = w_ukv.shape[0]
    w = w_ukv.reshape(n_layers, MLA_KV_RANK, MLA_HEADS, MLA_NOPE + MLA_V)
    k_nope, v = w[..., :MLA_NOPE], w[..., MLA_NOPE:]
    wk = jnp.concatenate([k_nope, jnp.zeros_like(k_nope)], axis=-1)
    return (wk.reshape(n_layers, MLA_KV_RANK, MLA_HEADS * LANES).astype(BF16),
            v.reshape(n_layers, MLA_KV_RANK, MLA_HEADS * MLA_V).astype(BF16))


def _rope_tables(seq_len):
    half = MLA_ROPE // 2
    inv = ROPE_BASE ** (-jnp.arange(half, dtype=F32) / half)
    ang = jnp.arange(seq_len).astype(F32)[:, None] * inv[None, :]
    cos, sin = jnp.cos(ang), jnp.sin(ang)
    ones = jnp.ones((seq_len, MLA_NOPE), F32)
    tail = jnp.zeros((seq_len, LANES - MLA_QK), F32)
    cos_t = jnp.concatenate([ones, cos, cos, tail], axis=-1)
    sin_t = jnp.concatenate([jnp.zeros_like(ones), sin, sin, tail], axis=-1)
    return cos_t, sin_t


def _band_bias(rel_table):
    r = jnp.arange(BAND_TQ)[:, None]
    c = jnp.arange(BAND_KEYS)[None, :]
    dist = r + LEFT_CHUNKS * CHUNK - c
    idx = jnp.clip(dist, -REL_CLIP, REL_CLIP) + REL_CLIP
    rc, cc = r // CHUNK, c // CHUNK
    in_band = (cc >= rc) & (cc <= rc + LEFT_CHUNKS)
    return jnp.where(in_band[None, None], rel_table[:, :, idx].astype(F32), NEG_INF)


def kernel(x, w_in, pre_g, post_g, sgu_ln_g, sgu_ln_b, sgu_w, sgu_b, mla_q_norm_g, mla_kv_norm_g, mla_w_uq,
           mla_w_ukv, ca_rel_bias, w_branch, gate_b, w_out):
    batch, seq_len, d = x.shape
    n_layers = w_in.shape[0]
    n_tok = batch * seq_len
    assert d == D_MODEL and seq_len % MLA_TQ == 0 and seq_len % BACK_TM == 0

    w_all = _prep_w_in(w_in)
    wuq, wuqs = _prep_w_uq(mla_w_uq)
    wk, wv = _prep_w_ukv(mla_w_ukv)
    cos_t, sin_t = _rope_tables(seq_len)
    bias = _band_bias(ca_rel_bias)
    sgub_full = jnp.repeat(jnp.swapaxes(sgu_b, 1, 2), SGU_GROUP_DIM, axis=-1).astype(F32)
    wbr = w_branch.astype(BF16)
    wout = w_out.astype(BF16)
    row = lambda a, l: a[l].reshape(1, -1).astype(F32)

    x2 = x.reshape(n_tok, d)
    for l in range(n_layers):
        ya, q, k, v, szb, qc, kc, vc, szc, gates = _front_call(
            x2, w_all[l], row(pre_g, l), row(sgu_ln_g, l), row(sgu_ln_b, l), sgu_w[l], sgub_full[l],
            row(mla_q_norm_g, l), row(mla_kv_norm_g, l), wuq[l], wuqs[l], wk[l], wv[l], row(gate_b, l),
            cos_t, sin_t, seq_len)
        seq = lambda a: a.reshape(batch, seq_len, a.shape[-1])
        yb = _mla_call(seq(q), seq(k), seq(v), seq(szb))
        yc = _band_call(seq(qc), seq(kc), seq(vc), bias[l], seq(szc))
        x2 = _back_call(x2, ya, yb.reshape(n_tok, BR_WIDTH), yc.reshape(n_tok, BR_WIDTH), gates,
                        wbr[l], wout[l], row(post_g, l))
    return x2.reshape(batch, seq_len, d)
```

```python
import math

import jax
import jax.numpy as jnp
from jax import lax
from jax.experimental import pallas as pl
from jax.experimental.pallas import tpu as pltpu

F32 = jnp.float32
BF16 = jnp.bfloat16

D_MODEL = 1024
CHUNK = 64
EPS = 1e-6
NEG_INF = -1e30
N_BRANCH = 3
BR_WIDTH = 512

SGU_BLOCK = 128
SGU_GROUPS = 8
SGU_GROUP_DIM = BR_WIDTH // SGU_GROUPS

MLA_HEADS = 8
MLA_NOPE = 64
MLA_ROPE = 32
MLA_V = 64
MLA_QK = MLA_NOPE + MLA_ROPE
MLA_Q_RANK = 256
MLA_KV_RANK = 128
ROPE_BASE = 10000.0

CA_HEADS = 8
CA_HEAD_DIM = BR_WIDTH // CA_HEADS
LEFT_CHUNKS = 8
REL_CLIP = 128

LANES = 128
HEAD_PAIRS = MLA_HEADS // 2
LOG2E = math.log2(math.e)

COL_SGU = 0
COL_MLA = 1536
COL_ZB = 2048
COL_BAND = 2560
COL_GATE = 4096
W_COLS = 7168

ATT_T = 256
MLA_T = 512
MLA_MASK_LANE = MLA_QK
MASK_BIG = 1e30
FRONT_TM = ATT_T
BACK_TM = 512
BAND_KEYS = 3 * ATT_T

VMEM_LIMIT_FRONT = 56 * 1024 * 1024
VMEM_LIMIT_OTHER = 48 * 1024 * 1024


def _const_spec(shape):
    zeros = (0,) * len(shape)
    return pl.BlockSpec(shape, lambda *_: zeros, pipeline_mode=pl.Buffered(1))


def _dot(a, b):
    return jnp.dot(a, b, preferred_element_type=F32)


def _dot_nt(a, b):
    return lax.dot_general(a, b, (((1,), (1,)), ((), ())), preferred_element_type=F32)


def _silu(z):
    return z * jax.nn.sigmoid(z)


def _front_kernel(x_ref, w_ref, preg_ref, lng_ref, lnb_ref, sguw_ref, sgub_ref, qg_ref, kvg_ref,
                  wuq_ref, wuqs_ref, wk_ref, wvt_ref, wvct_ref, gateb_ref, cos_ref, sin_ref, hot_ref,
                  ya_ref, q_ref, k_ref, vt_ref, szb_ref, qc_ref, kc_ref, vct_ref, szc_ref, g_ref):
    tm = x_ref.shape[0]
    x = x_ref[...]
    xn = x * lax.rsqrt(jnp.mean(x * x, axis=-1, keepdims=True) + EPS)
    xn = (xn * preg_ref[...]).astype(BF16)

    pa = _dot(xn, w_ref[:, COL_SGU:COL_SGU + 3 * BR_WIDTH])
    u = pa[:, 0:BR_WIDTH]
    v = pa[:, BR_WIDTH:2 * BR_WIDTH]
    za = pa[:, 2 * BR_WIDTH:3 * BR_WIDTH]
    mu = jnp.mean(v, axis=-1, keepdims=True)
    vc = v - mu
    vn = vc * lax.rsqrt(jnp.mean(vc * vc, axis=-1, keepdims=True) + EPS)
    vn = (vn * lng_ref[...] + lnb_ref[...]).astype(BF16)

    row = lax.broadcasted_iota(jnp.int32, (SGU_BLOCK, SGU_BLOCK), 0)
    col = lax.broadcasted_iota(jnp.int32, (SGU_BLOCK, SGU_BLOCK), 1)
    tri = row >= col
    ws = [jnp.where(tri, sguw_ref[g], 0.0).astype(BF16) for g in range(SGU_GROUPS)]
    low_half = lax.broadcasted_iota(jnp.int32, (SGU_BLOCK, LANES), 1) < SGU_GROUP_DIM
    blocks = []
    for blk in range(tm // SGU_BLOCK):
        vb = vn[blk * SGU_BLOCK:(blk + 1) * SGU_BLOCK, :]
        tiles = []
        for j in range(BR_WIDTH // LANES):
            vt = vb[:, j * LANES:(j + 1) * LANES]
            tiles.append(jnp.where(low_half, _dot(ws[2 * j], vt), _dot(ws[2 * j + 1], vt)))
        blocks.append(jnp.concatenate(tiles, axis=1) + sgub_ref[...])
    mixed = jnp.concatenate(blocks, axis=0)
    ya_ref[...] = ((u * mixed) * _silu(za)).astype(BF16)

    pb = _dot(xn, w_ref[:, COL_MLA:COL_MLA + 512])
    qd = pb[:, 0:MLA_Q_RANK]
    kvd = pb[:, MLA_Q_RANK:MLA_Q_RANK + MLA_KV_RANK]
    kr = pb[:, MLA_Q_RANK + MLA_KV_RANK:512]
    cos_t = cos_ref[...]
    sin_t = sin_ref[...]
    cq = qd * lax.rsqrt(jnp.mean(qd * qd, axis=-1, keepdims=True) + EPS)
    cq = (cq * qg_ref[...]).astype(BF16)
    qf = _dot(cq, wuq_ref[...])
    qs = _dot(cq, wuqs_ref[...])
    ckv = kvd * lax.rsqrt(jnp.mean(kvd * kvd, axis=-1, keepdims=True) + EPS)
    ckv = (ckv * kvg_ref[...]).astype(BF16)
    kf = _dot(ckv, wk_ref[...])
    kr_roped = kr * cos_t + pltpu.roll(kr, LANES - MLA_ROPE, 1) * sin_t
    kr_roped = kr_roped + hot_ref[...]
    scale = (MLA_QK ** -0.5) * LOG2E
    for h in range(MLA_HEADS):
        sl = slice(h * LANES, (h + 1) * LANES)
        q_ref[:, sl] = ((qf[:, sl] * cos_t + qs[:, sl] * sin_t) * scale).astype(BF16)
        k_ref[:, sl] = (kf[:, sl] + kr_roped).astype(BF16)
    vt_ref[0, 0] = _dot_nt(wvt_ref[...], ckv).astype(BF16)
    szb_ref[...] = _silu(_dot(xn, w_ref[:, COL_ZB:COL_ZB + BR_WIDTH])).astype(BF16)

    pc = _dot(xn, w_ref[:, COL_BAND:COL_BAND + 3 * BR_WIDTH])
    qc_ref[...] = (pc[:, 0:BR_WIDTH] * ((CA_HEAD_DIM ** -0.5) * LOG2E)).astype(BF16)
    kc_ref[...] = pc[:, BR_WIDTH:2 * BR_WIDTH].astype(BF16)
    szc_ref[...] = _silu(pc[:, 2 * BR_WIDTH:3 * BR_WIDTH]).astype(BF16)
    vct_ref[0, 0] = _dot_nt(wvct_ref[...], xn).astype(BF16)

    pg = _dot(xn, w_ref[:, COL_GATE:COL_GATE + N_BRANCH * D_MODEL]) + gateb_ref[...]
    g_ref[...] = jax.nn.sigmoid(pg).astype(BF16)


def _front_call(x2, w, preg, lng, lnb, sguw, sgub_full, qg, kvg, wuq, wuqs, wk, wvt, wvct, gateb,
                cos_t, sin_t, hot_t, batch, seq_len):
    n_tok = x2.shape[0]
    tm = FRONT_TM
    tiles_per_seq = seq_len // tm
    tok = lambda width: pl.BlockSpec((tm, width), lambda i: (i, 0))
    pos = pl.BlockSpec((tm, LANES), lambda i: (i % tiles_per_seq, 0))
    tposed = pl.BlockSpec((1, 1, BR_WIDTH, tm), lambda i: (i // tiles_per_seq, i % tiles_per_seq, 0, 0))
    tok_out = lambda width: jax.ShapeDtypeStruct((n_tok, width), BF16)
    tposed_out = jax.ShapeDtypeStruct((batch, tiles_per_seq, BR_WIDTH, tm), BF16)
    wide = MLA_HEADS * LANES
    consts = (w, preg, lng, lnb, sguw, sgub_full, qg, kvg, wuq, wuqs, wk, wvt, wvct, gateb)
    return pl.pallas_call(
        _front_kernel,
        grid=(n_tok // tm,),
        in_specs=[tok(D_MODEL)] + [_const_spec(c.shape) for c in consts] + [pos, pos, pos],
        out_specs=[tok(BR_WIDTH), tok(wide), tok(wide), tposed, tok(BR_WIDTH),
                   tok(BR_WIDTH), tok(BR_WIDTH), tposed, tok(BR_WIDTH), tok(N_BRANCH * D_MODEL)],
        out_shape=[tok_out(BR_WIDTH), tok_out(wide), tok_out(wide), tposed_out, tok_out(BR_WIDTH),
                   tok_out(BR_WIDTH), tok_out(BR_WIDTH), tposed_out, tok_out(BR_WIDTH),
                   tok_out(N_BRANCH * D_MODEL)],
        compiler_params=pltpu.CompilerParams(dimension_semantics=("arbitrary",),
                                             vmem_limit_bytes=VMEM_LIMIT_FRONT),
        name="front",
    )(x2, *consts, cos_t, sin_t, hot_t)


def _next_item(i, j, n_tiles):
    row_done = j >= i
    at_end = jnp.logical_and(row_done, i >= n_tiles - 1)
    ni = jnp.where(jnp.logical_and(row_done, jnp.logical_not(at_end)), i + 1, i)
    nj = jnp.where(at_end, j, jnp.where(row_done, 0, j + 1))
    return ni, nj


def _mla_kernel(q_ref, k_ref, vt_ref, sz_ref, qmask_ref, o_ref, s0_ref, s1_ref, p0_ref, p1_ref):
    t = MLA_T
    n_tiles = q_ref.shape[1] // t
    n_items = n_tiles * (n_tiles + 1) // 2
    n_half_steps = n_items + 2
    n_steps = (n_half_steps + 1) // 2
    sub = t // ATT_T

    s1_ref[...] = jnp.zeros(s1_ref.shape, F32)
    p0_ref[...] = jnp.zeros(p0_ref.shape, BF16)

    def scores(i, j, s_w):
        off_q = pl.multiple_of(i * t, t)
        off_k = pl.multiple_of(j * t, t)
        cmax = []
        for h in range(2):
            lanes = slice(h * LANES, (h + 1) * LANES)
            q = q_ref[0, pl.ds(off_q, t), lanes]
            q = jnp.where(i == j, q + qmask_ref[...], q)
            st = _dot_nt(k_ref[0, pl.ds(off_k, t), lanes], q)
            s_w[h] = st
            cmax.append(jnp.max(st, axis=0, keepdims=True))
        return cmax

    def softmax(j, cmax, m, l, s_r, p_w):
        first = j == 0
        new_m, new_l, alpha = [], [], []
        for h in range(2):
            m_old = jnp.where(first, NEG_INF, m[h])
            l_old = jnp.where(first, 0.0, l[h])
            m_new = jnp.maximum(m_old, cmax[h])
            a = jnp.exp2(m_old - m_new)
            p = jnp.exp2(s_r[h] - m_new)
            new_l.append(a * l_old + jnp.sum(p, axis=0, keepdims=True))
            p_w[h] = p.astype(BF16)
            new_m.append(m_new)
            alpha.append(a)
        return new_m, new_l, alpha

    def values(j, alpha, acc, p_r):
        first = j == 0
        new_acc = []
        for h in range(2):
            rows = slice(h * MLA_V, (h + 1) * MLA_V)
            vt = jnp.concatenate([vt_ref[0, sub * j + c, rows, :] for c in range(sub)], axis=1)
            new_acc.append(jnp.where(first, 0.0, alpha[h] * acc[h]) + _dot(vt, p_r[h]))
        return new_acc

    def write_tile(i, acc, l):
        off = pl.multiple_of(i * t, t)
        ot = jnp.concatenate([acc[0] / l[0], acc[1] / l[1]], axis=0)
        o_ref[0, pl.ds(off, t), :] = (ot.T * sz_ref[0, pl.ds(off, t), :].astype(F32)).astype(BF16)

    def half_step(half_idx, state, s_w, s_r, p_w, p_r):
        (i1, j1, i2, j2, i3, j3), cmax, m, l, alpha, acc = state
        new_cmax = scores(i1, j1, s_w)
        new_acc = values(j3, alpha, acc, p_r)
        new_m, new_l, new_alpha = softmax(j2, cmax, m, l, s_r, p_w)
        ni, nj = _next_item(i1, j1, n_tiles)
        row_done = jnp.logical_and(jnp.logical_and(half_idx >= 2, half_idx < n_half_steps), j3 == i3)
        done = (row_done, i3, new_acc, l)
        return ((ni, nj, i1, j1, i2, j2), new_cmax, new_m, new_l, new_alpha, new_acc), done

    def step(u, state):
        state, done_a = half_step(2 * u, state, s0_ref, s1_ref, p1_ref, p0_ref)
        state, done_b = half_step(2 * u + 1, state, s1_ref, s0_ref, p0_ref, p1_ref)
        for row_done, i, acc, l in (done_a, done_b):
            pl.when(row_done)(lambda i=i, acc=acc, l=l: write_tile(i, acc, l))
        return state

    zero = jnp.int32(0)
    vec = lambda val: [jnp.full((1, t), val, F32) for _ in range(2)]
    init = ((zero,) * 6, vec(0.0), vec(NEG_INF), vec(0.0), vec(0.0),
            [jnp.zeros((MLA_V, t), F32) for _ in range(2)])
    lax.fori_loop(0, n_steps, step, init)


def _mla_call(q, k, vt, szb, qmask):
    batch, seq_len, _ = q.shape
    t = MLA_T
    row = lambda width: pl.BlockSpec((1, seq_len, width), lambda b, hp: (b, 0, hp))
    return pl.pallas_call(
        _mla_kernel,
        grid=(batch, HEAD_PAIRS),
        in_specs=[row(2 * LANES), row(2 * LANES),
                  pl.BlockSpec((1, seq_len // ATT_T, LANES, ATT_T), lambda b, hp: (b, 0, hp, 0)),
                  row(LANES), _const_spec(qmask.shape)],
        out_specs=row(LANES),
        out_shape=jax.ShapeDtypeStruct((batch, seq_len, BR_WIDTH), BF16),
        scratch_shapes=[pltpu.VMEM((2, t, t), F32), pltpu.VMEM((2, t, t), F32),
                        pltpu.VMEM((2, t, t), BF16), pltpu.VMEM((2, t, t), BF16)],
        compiler_params=pltpu.CompilerParams(dimension_semantics=("arbitrary",) * 2,
                                             vmem_limit_bytes=VMEM_LIMIT_OTHER),
        name="mla",
    )(q, k, vt, szb, qmask)


def _band_kernel(q_ref, k2_ref, k1_ref, k0_ref, vt2_ref, vt1_ref, vt0_ref, bias_ref, sz_ref, o_ref,
                 s_ref, p_ref):
    t = ATT_T
    low_half = lax.broadcasted_iota(jnp.int32, (t, LANES), 1) < CA_HEAD_DIM
    cmax = []
    for j in range(BR_WIDTH // LANES):
        sl = slice(j * LANES, (j + 1) * LANES)
        qp = q_ref[0, :, sl]
        kcat = jnp.concatenate([k2_ref[0, :, sl], k1_ref[0, :, sl], k0_ref[0, :, sl]], axis=0)
        for hh in range(2):
            h = 2 * j + hh
            keep = low_half if hh == 0 else jnp.logical_not(low_half)
            qm = jnp.where(keep, qp, jnp.zeros_like(qp))
            st = _dot_nt(kcat, qm) + bias_ref[0, h]
            s_ref[h] = st
            cmax.append(jnp.max(st, axis=0, keepdims=True))
    inv_l = []
    for h in range(CA_HEADS):
        p = jnp.exp2(s_ref[h] - cmax[h])
        inv_l.append(1.0 / jnp.sum(p, axis=0, keepdims=True))
        p_ref[h] = p.astype(BF16)
    outs = []
    for h in range(CA_HEADS):
        rows = slice(h * CA_HEAD_DIM, (h + 1) * CA_HEAD_DIM)
        vt = jnp.concatenate([vt2_ref[0, 0, rows, :], vt1_ref[0, 0, rows, :], vt0_ref[0, 0, rows, :]],
                             axis=1)
        outs.append(_dot(vt, p_ref[h]) * inv_l[h])
    ot = jnp.concatenate(outs, axis=0)
    o_ref[0] = (ot.T * sz_ref[0].astype(F32)).astype(BF16)


def _band_call(qc, kc, vct, bias, szc):
    batch, seq_len, _ = qc.shape
    t = ATT_T
    back = lambda n: (lambda b, i: (b, jnp.maximum(i - n, 0), 0))
    tok = lambda n: pl.BlockSpec((1, t, BR_WIDTH), back(n))
    tposed = lambda n: pl.BlockSpec((1, 1, BR_WIDTH, t), lambda b, i: (b, jnp.maximum(i - n, 0), 0, 0))
    return pl.pallas_call(
        _band_kernel,
        grid=(batch, seq_len // t),
        in_specs=[tok(0), tok(2), tok(1), tok(0), tposed(2), tposed(1), tposed(0),
                  pl.BlockSpec((1,) + bias.shape[1:], lambda b, i: (jnp.maximum(2 - i, 0), 0, 0, 0)),
                  tok(0)],
        out_specs=tok(0),
        out_shape=jax.ShapeDtypeStruct((batch, seq_len, BR_WIDTH), BF16),
        scratch_shapes=[pltpu.VMEM((CA_HEADS, BAND_KEYS, t), F32), pltpu.VMEM((CA_HEADS, BAND_KEYS, t), BF16)],
        compiler_params=pltpu.CompilerParams(dimension_semantics=("arbitrary",) * 2,
                                             vmem_limit_bytes=VMEM_LIMIT_OTHER),
        name="band",
    )(qc, kc, kc, kc, vct, vct, vct, bias, szc)


def _back_kernel(x_ref, ya_ref, yb_ref, yc_ref, g_ref, wbr_ref, wout_ref, postg_ref, o_ref):
    merged = None
    for n, y_ref in enumerate((ya_ref, yb_ref, yc_ref)):
        br = _dot(y_ref[...], wbr_ref[n])
        term = g_ref[:, n * D_MODEL:(n + 1) * D_MODEL].astype(F32) * br
        merged = term if merged is None else merged + term
    out = _dot(merged.astype(BF16), wout_ref[...])
    y = out * lax.rsqrt(jnp.mean(out * out, axis=-1, keepdims=True) + EPS)
    o_ref[...] = x_ref[...] + y * postg_ref[...]


def _back_call(x2, ya, yb, yc, gates, wbr, wout, postg):
    n_tok = x2.shape[0]
    tm = BACK_TM
    tok = lambda width: pl.BlockSpec((tm, width), lambda i: (i, 0))
    return pl.pallas_call(
        _back_kernel,
        grid=(n_tok // tm,),
        in_specs=[tok(D_MODEL), tok(BR_WIDTH), tok(BR_WIDTH), tok(BR_WIDTH), tok(N_BRANCH * D_MODEL),
                  _const_spec(wbr.shape), _const_spec(wout.shape), _const_spec(postg.shape)],
        out_specs=tok(D_MODEL),
        out_shape=jax.ShapeDtypeStruct((n_tok, D_MODEL), F32),
        compiler_params=pltpu.CompilerParams(dimension_semantics=("arbitrary",),
                                             vmem_limit_bytes=VMEM_LIMIT_OTHER),
        name="back",
    )(x2, ya, yb, yc, gates, wbr, wout, postg)


def _rope_partner(w):
    half = w.shape[-1] // 2
    return jnp.concatenate([-w[..., half:], w[..., :half]], axis=-1)


def _prep_w_in(w_in):
    n_layers, d, _ = w_in.shape
    off_kr = 3 * BR_WIDTH + MLA_Q_RANK + MLA_KV_RANK
    off_zb = off_kr + MLA_ROPE
    off_qc = off_zb + BR_WIDTH
    off_vc = off_qc + 2 * BR_WIDTH
    off_zc = off_vc + BR_WIDTH
    w_kr = w_in[:, :, off_kr:off_zb]
    rope_tile = jnp.concatenate([jnp.zeros((n_layers, d, LANES - 2 * MLA_ROPE), w_in.dtype),
                                 w_kr, _rope_partner(w_kr)], axis=-1)
    w = jnp.concatenate([w_in[:, :, :off_kr], rope_tile, w_in[:, :, off_zb:off_vc], w_in[:, :, off_zc:]],
                        axis=-1)
    assert w.shape[-1] == W_COLS
    wvct = jnp.swapaxes(w_in[:, :, off_vc:off_zc], 1, 2)
    return w.astype(BF16), wvct.astype(BF16)


def _prep_w_uq(w_uq):
    n_layers = w_uq.shape[0]
    w = w_uq.reshape(n_layers, MLA_Q_RANK, MLA_HEADS, MLA_QK)
    nope, rope = w[..., :MLA_NOPE], w[..., MLA_NOPE:]
    pad = jnp.zeros(w.shape[:3] + (LANES - MLA_QK,), w.dtype)
    main = jnp.concatenate([nope, rope, pad], axis=-1)
    partner = jnp.concatenate([jnp.zeros_like(nope), _rope_partner(rope), pad], axis=-1)
    shape = (n_layers, MLA_Q_RANK, MLA_HEADS * LANES)
    return main.reshape(shape).astype(BF16), partner.reshape(shape).astype(BF16)


def _prep_w_ukv(w_ukv):
    n_layers = w_ukv.shape[0]
    w = w_ukv.reshape(n_layers, MLA_KV_RANK, MLA_HEADS, MLA_NOPE + MLA_V)
    k_nope, v = w[..., :MLA_NOPE], w[..., MLA_NOPE:]
    wk = jnp.concatenate([k_nope, jnp.zeros_like(k_nope)], axis=-1)
    wvt = jnp.swapaxes(v.reshape(n_layers, MLA_KV_RANK, MLA_HEADS * MLA_V), 1, 2)
    return wk.reshape(n_layers, MLA_KV_RANK, MLA_HEADS * LANES).astype(BF16), wvt.astype(BF16)


def _rope_tables(seq_len):
    half = MLA_ROPE // 2
    inv = ROPE_BASE ** (-jnp.arange(half, dtype=F32) / half)
    ang = jnp.arange(seq_len).astype(F32)[:, None] * inv[None, :]
    cos, sin = jnp.cos(ang), jnp.sin(ang)
    ones = jnp.ones((seq_len, MLA_NOPE), F32)
    tail = jnp.zeros((seq_len, LANES - MLA_QK), F32)
    cos_t = jnp.concatenate([ones, cos, cos, tail], axis=-1)
    sin_t = jnp.concatenate([jnp.zeros_like(ones), sin, sin, tail], axis=-1)
    return cos_t, sin_t


def _mask_tables(seq_len):
    chunks = MLA_T // CHUNK
    lane_chunk = jnp.arange(LANES) - MLA_MASK_LANE
    is_mask_lane = (lane_chunk >= 0) & (lane_chunk < chunks)
    pos_chunk = (jnp.arange(seq_len) // CHUNK) % chunks
    hot = jnp.where(is_mask_lane[None, :] & (lane_chunk[None, :] == pos_chunk[:, None]), 1.0, 0.0)
    row_chunk = jnp.arange(MLA_T) // CHUNK
    qmask = jnp.where(is_mask_lane[None, :] & (lane_chunk[None, :] > row_chunk[:, None]), -MASK_BIG, 0.0)
    return hot.astype(F32), qmask.astype(BF16)


def _band_bias(rel_table):
    n_keys, n_q = BAND_KEYS, ATT_T
    span = n_keys + n_q
    lo = n_q - 1 - REL_CLIP
    hi = span - 1 - lo - (2 * REL_CLIP + 1)
    lead = rel_table.shape[:-1]
    ext = jnp.concatenate([jnp.broadcast_to(rel_table[..., :1], lead + (lo,)), rel_table,
                           jnp.broadcast_to(rel_table[..., -1:], lead + (hi,)),
                           jnp.zeros(lead + (1,), rel_table.dtype)], axis=-1)
    flat = jnp.tile(ext, (1,) * len(lead) + (n_keys,))[..., :n_keys * (span - 1)]
    toep = flat.reshape(lead + (n_keys, span - 1))[..., n_keys - 1:n_keys - 1 + n_q]
    c = jnp.arange(n_keys)[:, None] // CHUNK
    r = jnp.arange(n_q)[None, :] // CHUNK
    in_band = (c >= r) & (c <= r + LEFT_CHUNKS)
    bias = jnp.where(in_band, toep.astype(F32) * LOG2E, NEG_INF)
    key_tile = jnp.arange(n_keys)[:, None] // n_q
    missing = jnp.arange(BAND_KEYS // n_q)[:, None, None, None] > key_tile
    return jnp.where(missing[None], NEG_INF, bias[:, None])


def kernel(x, w_in, pre_g, post_g, sgu_ln_g, sgu_ln_b, sgu_w, sgu_b, mla_q_norm_g, mla_kv_norm_g, mla_w_uq,
           mla_w_ukv, ca_rel_bias, w_branch, gate_b, w_out):
    batch, seq_len, d = x.shape
    n_layers = w_in.shape[0]
    n_tok = batch * seq_len
    assert d == D_MODEL and seq_len % MLA_T == 0 and MLA_T % BACK_TM == 0 and MLA_T % ATT_T == 0
    assert MLA_MASK_LANE + MLA_T // CHUNK <= LANES

    w_all, wvct = _prep_w_in(w_in)
    wuq, wuqs = _prep_w_uq(mla_w_uq)
    wk, wvt = _prep_w_ukv(mla_w_ukv)
    cos_t, sin_t = _rope_tables(seq_len)
    hot_t, qmask = _mask_tables(seq_len)
    bias = _band_bias(ca_rel_bias)
    sgub_full = jnp.repeat(jnp.swapaxes(sgu_b, 1, 2), SGU_GROUP_DIM, axis=-1).astype(F32)
    wbr = w_branch.astype(BF16)
    wout = w_out.astype(BF16)
    row = lambda a, l: a[l].reshape(1, -1).astype(F32)
    seq = lambda a: a.reshape(batch, seq_len, a.shape[-1])

    x2 = x.reshape(n_tok, d)
    for l in range(n_layers):
        ya, q, k, vt, szb, qc, kc, vct, szc, gates = _front_call(
            x2, w_all[l], row(pre_g, l), row(sgu_ln_g, l), row(sgu_ln_b, l), sgu_w[l], sgub_full[l],
            row(mla_q_norm_g, l), row(mla_kv_norm_g, l), wuq[l], wuqs[l], wk[l], wvt[l], wvct[l],
            row(gate_b, l), cos_t, sin_t, hot_t, batch, seq_len)
        yb = _mla_call(seq(q), seq(k), vt, seq(szb), qmask)
        yc = _band_call(seq(qc), seq(kc), vct, bias[l], seq(szc))
        x2 = _back_call(x2, ya, yb.reshape(n_tok, BR_WIDTH), yc.reshape(n_tok, BR_WIDTH), gates,
                        wbr[l], wout[l], row(post_g, l))
    return x2.reshape(batch, seq_len, d)
```

```python
import math

import jax
import jax.numpy as jnp
from jax import lax
from jax.experimental import pallas as pl
from jax.experimental.pallas import tpu as pltpu

F32 = jnp.float32
BF16 = jnp.bfloat16

D_MODEL = 1024
CHUNK = 64
EPS = 1e-6
NEG_INF = -1e30
N_BRANCH = 3
BR_WIDTH = 512

SGU_BLOCK = 128
SGU_GROUPS = 8
SGU_GROUP_DIM = BR_WIDTH // SGU_GROUPS

MLA_HEADS = 8
MLA_NOPE = 64
MLA_ROPE = 32
MLA_V = 64
MLA_QK = MLA_NOPE + MLA_ROPE
MLA_Q_RANK = 256
MLA_KV_RANK = 128
ROPE_BASE = 10000.0

CA_HEADS = 8
CA_HEAD_DIM = BR_WIDTH // CA_HEADS
LEFT_CHUNKS = 8
REL_CLIP = 128

LANES = 128
HEAD_PAIRS = MLA_HEADS // 2
LOG2E = math.log2(math.e)

ATT_T = 256
MLA_T = 512
MLA_MASK_LANE = MLA_QK
MLA_SLOTS = 2
MLA_UNROLL = 4
MLA_SUM_ROWS = 16
MASK_BIG = 1e30
FRONT_TM = ATT_T
BACK_TM = 512
BAND_KEYS = 3 * ATT_T

VMEM_LIMIT_FRONT = 56 * 1024 * 1024
VMEM_LIMIT_OTHER = 48 * 1024 * 1024


def _const_spec(shape):
    zeros = (0,) * len(shape)
    return pl.BlockSpec(shape, lambda *_: zeros, pipeline_mode=pl.Buffered(1))


def _dot(a, b):
    return jnp.dot(a, b, preferred_element_type=F32)


def _dot_nt(a, b):
    return lax.dot_general(a, b, (((1,), (1,)), ((), ())), preferred_element_type=F32)


def _silu(z):
    return z * jax.nn.sigmoid(z)


def _front_kernel(x_ref, wa_ref, wb_ref, wzb_ref, wqk_ref, wzc_ref, wg_ref, preg_ref, lng_ref, lnb_ref, sguw_ref, sgub_ref, qg_ref, kvg_ref,
                  wuq_ref, wuqs_ref, wk_ref, wvt_ref, wvct_ref, gateb_ref, cos_ref, sin_ref, hot_ref,
                  ya_ref, q_ref, k_ref, vt_ref, szb_ref, qc_ref, kc_ref, vct_ref, szc_ref, g_ref):
    tm = x_ref.shape[0]
    x = x_ref[...]
    xn = x * lax.rsqrt(jnp.mean(x * x, axis=-1, keepdims=True) + EPS)
    xn = (xn * preg_ref[...]).astype(BF16)

    pa = _dot(xn, wa_ref[...])
    u = pa[:, 0:BR_WIDTH]
    v = pa[:, BR_WIDTH:2 * BR_WIDTH]
    za = pa[:, 2 * BR_WIDTH:3 * BR_WIDTH]
    mu = jnp.mean(v, axis=-1, keepdims=True)
    vc = v - mu
    vn = vc * lax.rsqrt(jnp.mean(vc * vc, axis=-1, keepdims=True) + EPS)
    vn = (vn * lng_ref[...] + lnb_ref[...]).astype(BF16)

    row = lax.broadcasted_iota(jnp.int32, (SGU_BLOCK, SGU_BLOCK), 0)
    col = lax.broadcasted_iota(jnp.int32, (SGU_BLOCK, SGU_BLOCK), 1)
    tri = row >= col
    ws = [jnp.where(tri, sguw_ref[g], 0.0).astype(BF16) for g in range(SGU_GROUPS)]
    low_half = lax.broadcasted_iota(jnp.int32, (SGU_BLOCK, LANES), 1) < SGU_GROUP_DIM
    blocks = []
    for blk in range(tm // SGU_BLOCK):
        vb = vn[blk * SGU_BLOCK:(blk + 1) * SGU_BLOCK, :]
        tiles = []
        for j in range(BR_WIDTH // LANES):
            vt = vb[:, j * LANES:(j + 1) * LANES]
            tiles.append(jnp.where(low_half, _dot(ws[2 * j], vt), _dot(ws[2 * j + 1], vt)))
        blocks.append(jnp.concatenate(tiles, axis=1) + sgub_ref[...])
    mixed = jnp.concatenate(blocks, axis=0)
    ya_ref[...] = ((u * mixed) * _silu(za)).astype(BF16)

    pb = _dot(xn, wb_ref[...])
    qd = pb[:, 0:MLA_Q_RANK]
    kvd = pb[:, MLA_Q_RANK:MLA_Q_RANK + MLA_KV_RANK]
    kr = pb[:, MLA_Q_RANK + MLA_KV_RANK:512]
    cos_t = cos_ref[...]
    sin_t = sin_ref[...]
    cq = qd * lax.rsqrt(jnp.mean(qd * qd, axis=-1, keepdims=True) + EPS)
    cq = (cq * qg_ref[...]).astype(BF16)
    qf = _dot(cq, wuq_ref[...])
    qs = _dot(cq, wuqs_ref[...])
    ckv = kvd * lax.rsqrt(jnp.mean(kvd * kvd, axis=-1, keepdims=True) + EPS)
    ckv = (ckv * kvg_ref[...]).astype(BF16)
    kf = _dot(ckv, wk_ref[...])
    kr_roped = kr * cos_t + pltpu.roll(kr, LANES - MLA_ROPE, 1) * sin_t
    kr_roped = kr_roped + hot_ref[...]
    scale = (MLA_QK ** -0.5) * LOG2E
    for h in range(MLA_HEADS):
        sl = slice(h * LANES, (h + 1) * LANES)
        q_ref[:, sl] = ((qf[:, sl] * cos_t + qs[:, sl] * sin_t) * scale).astype(BF16)
        k_ref[:, sl] = (kf[:, sl] + kr_roped).astype(BF16)
    vt_ref[0, 0] = _dot_nt(wvt_ref[...], ckv).astype(BF16)
    szb_ref[...] = _silu(_dot(xn, wzb_ref[...])).astype(BF16)

    pc = _dot(xn, wqk_ref[...])
    qc_ref[...] = (pc[:, 0:BR_WIDTH] * ((CA_HEAD_DIM ** -0.5) * LOG2E)).astype(BF16)
    kc_ref[...] = pc[:, BR_WIDTH:2 * BR_WIDTH].astype(BF16)
    szc_ref[...] = _silu(_dot(xn, wzc_ref[...])).astype(BF16)
    vct_ref[0, 0] = _dot_nt(wvct_ref[...], xn).astype(BF16)

    pg = _dot(xn, wg_ref[...]) + gateb_ref[...]
    g_ref[...] = jax.nn.sigmoid(pg).astype(BF16)


def _front_call(x2, w_parts, preg, lng, lnb, sguw, sgub_full, qg, kvg, wuq, wuqs, wk, wvt, wvct, gateb,
                cos_t, sin_t, hot_t, batch, seq_len):
    n_tok = x2.shape[0]
    tm = FRONT_TM
    tiles_per_seq = seq_len // tm
    tok = lambda width: pl.BlockSpec((tm, width), lambda i: (i, 0))
    pos = pl.BlockSpec((tm, LANES), lambda i: (i % tiles_per_seq, 0))
    tposed = pl.BlockSpec((1, 1, BR_WIDTH, tm), lambda i: (i // tiles_per_seq, i % tiles_per_seq, 0, 0))
    tok_out = lambda width: jax.ShapeDtypeStruct((n_tok, width), BF16)
    tposed_out = jax.ShapeDtypeStruct((batch, tiles_per_seq, BR_WIDTH, tm), BF16)
    wide = MLA_HEADS * LANES
    consts = (*w_parts, preg, lng, lnb, sguw, sgub_full, qg, kvg, wuq, wuqs, wk, wvt, wvct, gateb)
    return pl.pallas_call(
        _front_kernel,
        grid=(n_tok // tm,),
        in_specs=[tok(D_MODEL)] + [_const_spec(c.shape) for c in consts] + [pos, pos, pos],
        out_specs=[tok(BR_WIDTH), tok(wide), tok(wide), tposed, tok(BR_WIDTH),
                   tok(BR_WIDTH), tok(BR_WIDTH), tposed, tok(BR_WIDTH), tok(N_BRANCH * D_MODEL)],
        out_shape=[tok_out(BR_WIDTH), tok_out(wide), tok_out(wide), tposed_out, tok_out(BR_WIDTH),
                   tok_out(BR_WIDTH), tok_out(BR_WIDTH), tposed_out, tok_out(BR_WIDTH),
                   tok_out(N_BRANCH * D_MODEL)],
        compiler_params=pltpu.CompilerParams(dimension_semantics=("arbitrary",),
                                             vmem_limit_bytes=VMEM_LIMIT_FRONT),
        name="front",
    )(x2, *consts, cos_t, sin_t, hot_t)


def _next_item(i, j, n_tiles):
    row_done = j >= i
    at_end = jnp.logical_and(row_done, i >= n_tiles - 1)
    ni = jnp.where(jnp.logical_and(row_done, jnp.logical_not(at_end)), i + 1, i)
    nj = jnp.where(at_end, j, jnp.where(row_done, 0, j + 1))
    return ni, nj


def _mla_kernel(q_ref, k_ref, vt_ref, sz_ref, qmask_ref, o_ref, *scratch):
    t = MLA_T
    n_tiles = q_ref.shape[1] // t
    n_items = n_tiles * (n_tiles + 1) // 2
    n_half_steps = n_items + 2
    n_steps = -(-n_half_steps // MLA_UNROLL)
    sub = t // ATT_T
    s_refs = scratch[:MLA_SLOTS]
    p_refs = scratch[MLA_SLOTS:]

    s_refs[-1 % MLA_SLOTS][...] = jnp.zeros(s_refs[0].shape, F32)
    p_refs[-2 % MLA_SLOTS][...] = jnp.zeros(p_refs[0].shape, BF16)
    ones_rows = jnp.ones((MLA_SUM_ROWS, t), BF16)

    def scores(i, j, s_w):
        off_q = pl.multiple_of(i * t, t)
        off_k = pl.multiple_of(j * t, t)
        cmax = []
        for h in range(2):
            lanes = slice(h * LANES, (h + 1) * LANES)
            q = q_ref[0, pl.ds(off_q, t), lanes]
            q = jnp.where(i == j, q + qmask_ref[...], q)
            st = _dot_nt(k_ref[0, pl.ds(off_k, t), lanes], q)
            s_w[h] = st
            cmax.append(jnp.max(st, axis=0, keepdims=True))
        return cmax

    def softmax(j, cmax, m, s_r, p_w):
        first = j == 0
        new_m, alpha = [], []
        for h in range(2):
            m_old = jnp.where(first, NEG_INF, m[h])
            m_new = jnp.maximum(m_old, cmax[h])
            alpha.append(jnp.exp2(m_old - m_new))
            p_w[h] = jnp.exp2(s_r[h] - m_new).astype(BF16)
            new_m.append(m_new)
        return new_m, alpha

    def values(j, alpha, acc, p_r):
        first = j == 0
        new_acc = []
        for h in range(2):
            rows = slice(h * MLA_V, (h + 1) * MLA_V)
            vt = jnp.concatenate([vt_ref[0, sub * j + c, rows, :] for c in range(sub)], axis=1)
            vt = jnp.concatenate([vt, ones_rows], axis=0)
            new_acc.append(jnp.where(first, 0.0, alpha[h] * acc[h]) + _dot(vt, p_r[h]))
        return new_acc

    def write_tile(i, acc):
        off = pl.multiple_of(i * t, t)
        ot = jnp.concatenate([a[:MLA_V] / a[MLA_V:MLA_V + 1] for a in acc], axis=0)
        o_ref[0, pl.ds(off, t), :] = (ot.T * sz_ref[0, pl.ds(off, t), :].astype(F32)).astype(BF16)

    def half_step(half_idx, state, slot):
        (i1, j1, i2, j2, i3, j3), cmax, m, alpha, acc = state
        prev, prev2 = (slot - 1) % MLA_SLOTS, (slot - 2) % MLA_SLOTS
        new_cmax = scores(i1, j1, s_refs[slot])
        new_acc = values(j3, alpha, acc, p_refs[prev2])
        new_m, new_alpha = softmax(j2, cmax, m, s_refs[prev], p_refs[prev])
        ni, nj = _next_item(i1, j1, n_tiles)
        row_done = jnp.logical_and(jnp.logical_and(half_idx >= 2, half_idx < n_half_steps), j3 == i3)
        return ((ni, nj, i1, j1, i2, j2), new_cmax, new_m, new_alpha, new_acc), (row_done, i3, new_acc)

    def step(u, state):
        done = []
        for c in range(MLA_UNROLL):
            state, d = half_step(MLA_UNROLL * u + c, state, c % MLA_SLOTS)
            done.append(d)
        for row_done, i, acc in done:
            pl.when(row_done)(lambda i=i, acc=acc: write_tile(i, acc))
        return state

    zero = jnp.int32(0)
    vec = lambda val: [jnp.full((1, t), val, F32) for _ in range(2)]
    init = ((zero,) * 6, vec(0.0), vec(NEG_INF), vec(0.0),
            [jnp.zeros((MLA_V + MLA_SUM_ROWS, t), F32) for _ in range(2)])
    lax.fori_loop(0, n_steps, step, init)


def _mla_call(q, k, vt, szb, qmask):
    batch, seq_len, _ = q.shape
    t = MLA_T
    row = lambda width: pl.BlockSpec((1, seq_len, width), lambda b, hp: (b, 0, hp))
    return pl.pallas_call(
        _mla_kernel,
        grid=(batch, HEAD_PAIRS),
        in_specs=[row(2 * LANES), row(2 * LANES),
                  pl.BlockSpec((1, seq_len // ATT_T, LANES, ATT_T), lambda b, hp: (b, 0, hp, 0)),
                  row(LANES), _const_spec(qmask.shape)],
        out_specs=row(LANES),
        out_shape=jax.ShapeDtypeStruct((batch, seq_len, BR_WIDTH), BF16),
        scratch_shapes=([pltpu.VMEM((2, t, t), F32)] * MLA_SLOTS + [pltpu.VMEM((2, t, t), BF16)] * MLA_SLOTS),
        compiler_params=pltpu.CompilerParams(dimension_semantics=("arbitrary",) * 2,
                                             vmem_limit_bytes=VMEM_LIMIT_OTHER),
        name="mla",
    )(q, k, vt, szb, qmask)


def _band_kernel(q_ref, k2_ref, k1_ref, k0_ref, vt2_ref, vt1_ref, vt0_ref, bias_ref, sz_ref, o_ref,
                 s_ref, p_ref):
    t = ATT_T
    low_half = lax.broadcasted_iota(jnp.int32, (t, LANES), 1) < CA_HEAD_DIM
    cmax = []
    for j in range(BR_WIDTH // LANES):
        sl = slice(j * LANES, (j + 1) * LANES)
        qp = q_ref[0, :, sl]
        kcat = jnp.concatenate([k2_ref[0, :, sl], k1_ref[0, :, sl], k0_ref[0, :, sl]], axis=0)
        for hh in range(2):
            h = 2 * j + hh
            keep = low_half if hh == 0 else jnp.logical_not(low_half)
            qm = jnp.where(keep, qp, jnp.zeros_like(qp))
            st = _dot_nt(kcat, qm) + bias_ref[0, h]
            s_ref[h] = st
            cmax.append(jnp.max(st, axis=0, keepdims=True))
    for h in range(CA_HEADS):
        p_ref[h] = jnp.exp2(s_ref[h] - cmax[h]).astype(BF16)
    ones_rows = jnp.ones((MLA_SUM_ROWS, BAND_KEYS), BF16)
    outs = []
    for h in range(CA_HEADS):
        rows = slice(h * CA_HEAD_DIM, (h + 1) * CA_HEAD_DIM)
        vt = jnp.concatenate([vt2_ref[0, 0, rows, :], vt1_ref[0, 0, rows, :], vt0_ref[0, 0, rows, :]],
                             axis=1)
        ov = _dot(jnp.concatenate([vt, ones_rows], axis=0), p_ref[h])
        outs.append(ov[:CA_HEAD_DIM] / ov[CA_HEAD_DIM:CA_HEAD_DIM + 1])
    ot = jnp.concatenate(outs, axis=0)
    o_ref[0] = (ot.T * sz_ref[0].astype(F32)).astype(BF16)


def _band_call(qc, kc, vct, bias, szc):
    batch, seq_len, _ = qc.shape
    t = ATT_T
    back = lambda n: (lambda b, i: (b, jnp.maximum(i - n, 0), 0))
    tok = lambda n: pl.BlockSpec((1, t, BR_WIDTH), back(n))
    tposed = lambda n: pl.BlockSpec((1, 1, BR_WIDTH, t), lambda b, i: (b, jnp.maximum(i - n, 0), 0, 0))
    return pl.pallas_call(
        _band_kernel,
        grid=(batch, seq_len // t),
        in_specs=[tok(0), tok(2), tok(1), tok(0), tposed(2), tposed(1), tposed(0),
                  pl.BlockSpec((1,) + bias.shape[1:], lambda b, i: (jnp.maximum(2 - i, 0), 0, 0, 0)),
                  tok(0)],
        out_specs=tok(0),
        out_shape=jax.ShapeDtypeStruct((batch, seq_len, BR_WIDTH), BF16),
        scratch_shapes=[pltpu.VMEM((CA_HEADS, BAND_KEYS, t), F32), pltpu.VMEM((CA_HEADS, BAND_KEYS, t), BF16)],
        compiler_params=pltpu.CompilerParams(dimension_semantics=("arbitrary",) * 2,
                                             vmem_limit_bytes=VMEM_LIMIT_OTHER),
        name="band",
    )(qc, kc, kc, kc, vct, vct, vct, bias, szc)


def _back_kernel(x_ref, ya_ref, yb_ref, yc_ref, g_ref, wbr_ref, wout_ref, postg_ref, o_ref):
    merged = None
    for n, y_ref in enumerate((ya_ref, yb_ref, yc_ref)):
        br = _dot(y_ref[...], wbr_ref[n])
        term = g_ref[:, n * D_MODEL:(n + 1) * D_MODEL].astype(F32) * br
        merged = term if merged is None else merged + term
    out = _dot(merged.astype(BF16), wout_ref[...])
    y = out * lax.rsqrt(jnp.mean(out * out, axis=-1, keepdims=True) + EPS)
    o_ref[...] = x_ref[...] + y * postg_ref[...]


def _back_call(x2, ya, yb, yc, gates, wbr, wout, postg):
    n_tok = x2.shape[0]
    tm = BACK_TM
    tok = lambda width: pl.BlockSpec((tm, width), lambda i: (i, 0))
    return pl.pallas_call(
        _back_kernel,
        grid=(n_tok // tm,),
        in_specs=[tok(D_MODEL), tok(BR_WIDTH), tok(BR_WIDTH), tok(BR_WIDTH), tok(N_BRANCH * D_MODEL),
                  _const_spec(wbr.shape), _const_spec(wout.shape), _const_spec(postg.shape)],
        out_specs=tok(D_MODEL),
        out_shape=jax.ShapeDtypeStruct((n_tok, D_MODEL), F32),
        compiler_params=pltpu.CompilerParams(dimension_semantics=("arbitrary",),
                                             vmem_limit_bytes=VMEM_LIMIT_OTHER),
        name="back",
    )(x2, ya, yb, yc, gates, wbr, wout, postg)


def _rope_partner(w):
    half = w.shape[-1] // 2
    return jnp.concatenate([-w[..., half:], w[..., :half]], axis=-1)


def _prep_w_in(w_in):
    n_layers, d, _ = w_in.shape
    off_kr = 3 * BR_WIDTH + MLA_Q_RANK + MLA_KV_RANK
    off_zb = off_kr + MLA_ROPE
    off_qc = off_zb + BR_WIDTH
    off_vc = off_qc + 2 * BR_WIDTH
    off_zc = off_vc + BR_WIDTH
    off_g = off_zc + BR_WIDTH
    w_kr = w_in[:, :, off_kr:off_zb]
    w_mla = jnp.concatenate([w_in[:, :, 3 * BR_WIDTH:off_kr],
                             jnp.zeros((n_layers, d, LANES - 2 * MLA_ROPE), w_in.dtype),
                             w_kr, _rope_partner(w_kr)], axis=-1)
    parts = (w_in[:, :, :3 * BR_WIDTH], w_mla, w_in[:, :, off_zb:off_qc], w_in[:, :, off_qc:off_vc],
             w_in[:, :, off_zc:off_g], w_in[:, :, off_g:])
    wvct = jnp.swapaxes(w_in[:, :, off_vc:off_zc], 1, 2)
    return tuple(p.astype(BF16) for p in parts), wvct.astype(BF16)


def _prep_w_uq(w_uq):
    n_layers = w_uq.shape[0]
    w = w_uq.reshape(n_layers, MLA_Q_RANK, MLA_HEADS, MLA_QK)
    nope, rope = w[..., :MLA_NOPE], w[..., MLA_NOPE:]
    pad = jnp.zeros(w.shape[:3] + (LANES - MLA_QK,), w.dtype)
    main = jnp.concatenate([nope, rope, pad], axis=-1)
    partner = jnp.concatenate([jnp.zeros_like(nope), _rope_partner(rope), pad], axis=-1)
    shape = (n_layers, MLA_Q_RANK, MLA_HEADS * LANES)
    return main.reshape(shape).astype(BF16), partner.reshape(shape).astype(BF16)


def _prep_w_ukv(w_ukv):
    n_layers = w_ukv.shape[0]
    w = w_ukv.reshape(n_layers, MLA_KV_RANK, MLA_HEADS, MLA_NOPE + MLA_V)
    k_nope, v = w[..., :MLA_NOPE], w[..., MLA_NOPE:]
    wk = jnp.concatenate([k_nope, jnp.zeros_like(k_nope)], axis=-1)
    wvt = jnp.swapaxes(v.reshape(n_layers, MLA_KV_RANK, MLA_HEADS * MLA_V), 1, 2)
    return wk.reshape(n_layers, MLA_KV_RANK, MLA_HEADS * LANES).astype(BF16), wvt.astype(BF16)


def _rope_tables(seq_len):
    half = MLA_ROPE // 2
    inv = ROPE_BASE ** (-jnp.arange(half, dtype=F32) / half)
    ang = jnp.arange(seq_len).astype(F32)[:, None] * inv[None, :]
    cos, sin = jnp.cos(ang), jnp.sin(ang)
    ones = jnp.ones((seq_len, MLA_NOPE), F32)
    tail = jnp.zeros((seq_len, LANES - MLA_QK), F32)
    cos_t = jnp.concatenate([ones, cos, cos, tail], axis=-1)
    sin_t = jnp.concatenate([jnp.zeros_like(ones), sin, sin, tail], axis=-1)
    return cos_t, sin_t


def _mask_tables(seq_len):
    chunks = MLA_T // CHUNK
    lane_chunk = jnp.arange(LANES) - MLA_MASK_LANE
    is_mask_lane = (lane_chunk >= 0) & (lane_chunk < chunks)
    pos_chunk = (jnp.arange(seq_len) // CHUNK) % chunks
    hot = jnp.where(is_mask_lane[None, :] & (lane_chunk[None, :] == pos_chunk[:, None]), 1.0, 0.0)
    row_chunk = jnp.arange(MLA_T) // CHUNK
    qmask = jnp.where(is_mask_lane[None, :] & (lane_chunk[None, :] > row_chunk[:, None]), -MASK_BIG, 0.0)
    return hot.astype(F32), qmask.astype(BF16)


def _band_bias(rel_table):
    n_keys, n_q = BAND_KEYS, ATT_T
    span = n_keys + n_q
    lo = n_q - 1 - REL_CLIP
    hi = span - 1 - lo - (2 * REL_CLIP + 1)
    lead = rel_table.shape[:-1]
    ext = jnp.concatenate([jnp.broadcast_to(rel_table[..., :1], lead + (lo,)), rel_table,
                           jnp.broadcast_to(rel_table[..., -1:], lead + (hi,)),
                           jnp.zeros(lead + (1,), rel_table.dtype)], axis=-1)
    flat = jnp.tile(ext, (1,) * len(lead) + (n_keys,))[..., :n_keys * (span - 1)]
    toep = flat.reshape(lead + (n_keys, span - 1))[..., n_keys - 1:n_keys - 1 + n_q]
    c = jnp.arange(n_keys)[:, None] // CHUNK
    r = jnp.arange(n_q)[None, :] // CHUNK
    in_band = (c >= r) & (c <= r + LEFT_CHUNKS)
    bias = jnp.where(in_band, toep.astype(F32) * LOG2E, NEG_INF)
    key_tile = jnp.arange(n_keys)[:, None] // n_q
    missing = jnp.arange(BAND_KEYS // n_q)[:, None, None, None] > key_tile
    return jnp.where(missing[None], NEG_INF, bias[:, None])


def kernel(x, w_in, pre_g, post_g, sgu_ln_g, sgu_ln_b, sgu_w, sgu_b, mla_q_norm_g, mla_kv_norm_g, mla_w_uq,
           mla_w_ukv, ca_rel_bias, w_branch, gate_b, w_out):
    batch, seq_len, d = x.shape
    n_layers = w_in.shape[0]
    n_tok = batch * seq_len
    assert d == D_MODEL and seq_len % MLA_T == 0 and MLA_T % BACK_TM == 0 and MLA_T % ATT_T == 0
    assert MLA_MASK_LANE + MLA_T // CHUNK <= LANES

    w_parts, wvct = _prep_w_in(w_in)
    wuq, wuqs = _prep_w_uq(mla_w_uq)
    wk, wvt = _prep_w_ukv(mla_w_ukv)
    cos_t, sin_t = _rope_tables(seq_len)
    hot_t, qmask = _mask_tables(seq_len)
    bias = _band_bias(ca_rel_bias)
    sgub_full = jnp.repeat(jnp.swapaxes(sgu_b, 1, 2), SGU_GROUP_DIM, axis=-1).astype(F32)
    wbr = w_branch.astype(BF16)
    wout = w_out.astype(BF16)
    row = lambda a, l: a[l].reshape(1, -1).astype(F32)
    seq = lambda a: a.reshape(batch, seq_len, a.shape[-1])

    x2 = x.reshape(n_tok, d)
    for l in range(n_layers):
        ya, q, k, vt, szb, qc, kc, vct, szc, gates = _front_call(
            x2, [w[l] for w in w_parts], row(pre_g, l), row(sgu_ln_g, l), row(sgu_ln_b, l), sgu_w[l], sgub_full[l],
            row(mla_q_norm_g, l), row(mla_kv_norm_g, l), wuq[l], wuqs[l], wk[l], wvt[l], wvct[l],
            row(gate_b, l), cos_t, sin_t, hot_t, batch, seq_len)
        yb = _mla_call(seq(q), seq(k), vt, seq(szb), qmask)
        yc = _band_call(seq(qc), seq(kc), vct, bias[l], seq(szc))
        x2 = _back_call(x2, ya, yb.reshape(n_tok, BR_WIDTH), yc.reshape(n_tok, BR_WIDTH), gates,
                        wbr[l], wout[l], row(post_g, l))
    return x2.reshape(batch, seq_len, d)
```
